```python
import math
import jax
import jax.numpy as jnp
from jax import lax
import numpy as np

D_MODEL = 2048
BATCH = 8
SEQ = 2048
DEPTH = 2
DEC_BATCH = 32
DEC_SEQ = 1
PAST_LEN = 8192
PAGE_SIZE = 128

D_MIX = D_MODEL
HG_HEADS = 4
HG_DK = 128
HG_DV = 128
HG_WIDTH = HG_HEADS * HG_DV
HG_CHUNK = 64
CONV_CH = D_MIX // 4
CONV_W = 3
N_HEADS = 8
HEAD_DIM = 128
KV_HEADS = 2
GQA = N_HEADS // KV_HEADS
ATT_WIDTH = N_HEADS * HEAD_DIM
KV_WIDTH = KV_HEADS * HEAD_DIM
BLOCK = 64
SEL_TOPK = 16
SEL_QB = 64
WINDOW = 512
WIN_QB = 128
ROPE_THETA = 10000.0
FORCE_SCORE = 1.0e4
NEG_BIG = -1.0e30
TINY = 1.0e-30
D_FF = 5632
N_EXPERTS = 8
TOP_K = 2
MOE_BM = 128
N_DENSE = (DEPTH + 1) // 2
N_MOE = DEPTH // 2
EPS = 1e-6
IN_WIDTHS = (HG_WIDTH,) * 4 + (CONV_CH,) * 3 + (ATT_WIDTH,) + (KV_WIDTH,) * 6 + (3 * N_HEADS,)
IN_COLS = 4 * HG_WIDTH + 3 * CONV_CH + ATT_WIDTH + 6 * KV_WIDTH + 3 * N_HEADS

kernel_name = 'hybrid_hgrn2_shortconv_nsa_step'


def rmsnorm(x, g):
    xf = x.astype(jnp.float32)
    y = xf * lax.rsqrt(jnp.mean(xf * xf, axis=-1, keepdims=True) + EPS)
    return y.astype(x.dtype) * g


def rope(x, pos):
    half = HEAD_DIM // 2
    inv = jnp.exp(-math.log(ROPE_THETA) * jnp.arange(half, dtype=jnp.float32) / half)
    ang = pos.astype(jnp.float32)[:, None] * inv[None, :]
    cos = jnp.cos(ang)[None, :, None, :]
    sin = jnp.sin(ang)[None, :, None, :]
    xf = x.astype(jnp.float32)
    x1, x2 = xf[..., :half], xf[..., half:]
    return jnp.concatenate([x1 * cos - x2 * sin, x2 * cos + x1 * sin], axis=-1).astype(x.dtype)


def masked_softmax(s, mask):
    s = jnp.where(mask, s.astype(jnp.float32), NEG_BIG)
    m = jnp.max(s, axis=-1, keepdims=True)
    e = jnp.where(mask, jnp.exp(s - m), 0.0)
    return e / jnp.maximum(jnp.sum(e, axis=-1, keepdims=True), TINY)


def split_in(z):
    offs = [int(o) for o in np.cumsum(IN_WIDTHS)[:-1]]
    return jnp.split(z, offs, axis=-1)


def hgrn2(q, zf, i, g, lower, s0, norm_g):
    B, L = q.shape[:2]
    f32 = jnp.float32
    q = q.reshape(B, L, HG_HEADS, HG_DK).astype(f32)
    zf = zf.reshape(B, L, HG_HEADS, HG_DK).astype(f32)
    v = i.reshape(B, L, HG_HEADS, HG_DV).astype(f32)
    lb = lower.reshape(HG_HEADS, HG_DK).astype(f32)
    f = lb + (1.0 - lb) * jax.nn.sigmoid(zf)
    log_f = jnp.log(jnp.maximum(f, TINY))
    k = (1.0 - lb) * jax.nn.sigmoid(-zf)
    c = math.gcd(L, HG_CHUNK)
    nc = L // c

    def chunks(t):
        return t.reshape(B, nc, c, HG_HEADS, t.shape[-1]).transpose(1, 0, 3, 2, 4)

    causal = jnp.tril(jnp.ones((c, c), bool))[:, :, None]

    def step(S, inp):
        qc, lf, kc, vc = inp
        b = jnp.cumsum(lf, axis=2)
        o = jnp.einsum('bhtk,bhkv->bhtv', qc * jnp.exp(b), S)
        decay = jnp.exp(jnp.where(causal, b[:, :, :, None, :] - b[:, :, None, :, :], NEG_BIG))
        a = jnp.einsum('bhtk,bhtsk,bhsk->bhts', qc, decay, kc)
        o = o + jnp.einsum('bhts,bhsv->bhtv', a, vc)
        b_last = b[:, :, -1:, :]
        S = jnp.exp(b_last[:, :, 0, :, None]) * S + jnp.einsum('bhsk,bhsv->bhkv', kc * jnp.exp(b_last - b), vc)
        return S, o

    S, o = lax.scan(step, s0.astype(f32), (chunks(q), chunks(log_f), chunks(k), chunks(v)))
    o = o.transpose(1, 0, 3, 2, 4).reshape(B, L, HG_HEADS, HG_DV)
    o = rmsnorm(o, norm_g.astype(f32)) * jax.nn.silu(g.reshape(B, L, HG_HEADS, HG_DV).astype(f32))
    return o.reshape(B, L, HG_WIDTH).astype(g.dtype), S.astype(s0.dtype)


def short_conv(bg, cg, h, state, w):
    L = h.shape[1]
    u = cg * h
    up = jnp.concatenate([state.astype(u.dtype), u], axis=1)
    y = sum(w[j] * up[:, j:j + L] for j in range(CONV_W))
    return bg * y, up[:, up.shape[1] - (CONV_W - 1):]


def to_blocks(t):
    B, T = t.shape[:2]
    nb = -(-T // BLOCK)
    t = jnp.pad(t, ((0, 0), (0, nb * BLOCK - T), (0, 0), (0, 0)))
    return t.reshape(B, nb, BLOCK, KV_HEADS, HEAD_DIM)


def compressed_branch(q, k, v, w_pos, pos):
    B, L = q.shape[:2]
    kb, vb = to_blocks(k), to_blocks(v)
    nb = kb.shape[1]
    ck = jnp.einsum('bnlhd,l->bnhd', kb, w_pos[0])
    cv = jnp.einsum('bnlhd,l->bnhd', vb, w_pos[1])
    qg = q.reshape(B, L, KV_HEADS, GQA, HEAD_DIM)
    s = jnp.einsum('bqhgd,bnhd->bhgqn', qg, ck) * HEAD_DIM ** -0.5
    mask = ((jnp.arange(nb) + 1) * BLOCK - 1)[None, :] <= pos[:, None]
    p = masked_softmax(s, mask)
    o = jnp.einsum('bhgqn,bnhd->bqhgd', p.astype(cv.dtype), cv).reshape(B, L, N_HEADS, HEAD_DIM)
    return o, jnp.sum(p, axis=2)


def select_blocks(imp, pos):
    nb = imp.shape[-1]
    blk = jnp.arange(nb)[None, :]
    cur = (pos // BLOCK)[:, None]
    score = jnp.where((blk == cur) | (blk == 0), FORCE_SCORE, jnp.where(blk <= cur, imp, -1.0))
    return lax.top_k(score, min(SEL_TOPK, nb))[1]


def selected_branch(q, k, v, idx, pos):
    B, L = q.shape[:2]
    kb = to_blocks(k).transpose(0, 3, 1, 2, 4)
    vb = to_blocks(v).transpose(0, 3, 1, 2, 4)
    n_sel = idx.shape[-1]
    qb = math.gcd(L, SEL_QB)
    nq = L // qb
    qs = q.reshape(B, nq, qb, KV_HEADS, GQA, HEAD_DIM).transpose(1, 0, 3, 4, 2, 5)
    ids = idx.reshape(B, KV_HEADS, nq, qb, n_sel).transpose(2, 0, 1, 3, 4)
    ps = pos.reshape(nq, qb)
    bi = jnp.arange(B)[:, None, None, None]
    hi = jnp.arange(KV_HEADS)[None, :, None, None]

    def one(args):
        qc, ic, pc = args
        kg = kb[bi, hi, ic]
        vg = vb[bi, hi, ic].reshape(B, KV_HEADS, qb, n_sel * BLOCK, HEAD_DIM)
        kpos = ic[..., None] * BLOCK + jnp.arange(BLOCK)
        mask = (kpos <= pc[None, None, :, None, None]).reshape(B, KV_HEADS, 1, qb, n_sel * BLOCK)
        s = jnp.einsum('bhgqd,bhqkld->bhgqkl', qc, kg) * HEAD_DIM ** -0.5
        p = masked_softmax(s.reshape(B, KV_HEADS, GQA, qb, n_sel * BLOCK), mask)
        return jnp.einsum('bhgqm,bhqmd->bhgqd', p.astype(vg.dtype), vg)

    o = lax.map(one, (qs, ids, ps))
    return o.transpose(1, 0, 4, 2, 3, 5).reshape(B, L, N_HEADS, HEAD_DIM)


def window_branch(q, k_buf, v_buf, k_new, v_new, start, keep):
    B, L = q.shape[:2]
    front = jnp.zeros((B, WINDOW - k_buf.shape[1], KV_HEADS, HEAD_DIM), k_new.dtype)
    kp = jnp.concatenate([front, k_buf.astype(k_new.dtype), k_new], axis=1)
    vp = jnp.concatenate([front, v_buf.astype(v_new.dtype), v_new], axis=1)
    qb = math.gcd(L, WIN_QB)
    nq = L // qb
    slab = (jnp.arange(nq) * qb)[:, None] + jnp.arange(qb + WINDOW)[None, :]
    kpos = (start - WINDOW + slab)[:, None, :]
    qpos = (start + jnp.arange(L)).reshape(nq, qb, 1)
    mask = (kpos <= qpos) & (kpos > qpos - WINDOW) & (kpos >= 0)
    qg = q.reshape(B, nq, qb, KV_HEADS, GQA, HEAD_DIM)
    s = jnp.einsum('bnqhgd,bnkhd->bnhgqk', qg, kp[:, slab]) * HEAD_DIM ** -0.5
    p = masked_softmax(s, mask[None, :, None, None])
    o = jnp.einsum('bnhgqk,bnkhd->bnqhgd', p.astype(vp.dtype), vp[:, slab]).reshape(B, L, N_HEADS, HEAD_DIM)
    n = kp.shape[1]
    return o, kp[:, n - keep:], vp[:, n - keep:]


def swiglu(x, w1, w3, w2):
    return (jax.nn.silu(x @ w1) * (x @ w3)) @ w2


def moe_swiglu(x, router, w1, w3, w2):
    B, L, D = x.shape
    xt = x.reshape(-1, D)
    n = xt.shape[0]
    logits = (xt @ router).astype(jnp.float32)
    top_v, top_e = lax.top_k(logits, TOP_K)
    gates = jax.nn.softmax(top_v, axis=-1)
    a = n * TOP_K
    e_flat = top_e.reshape(-1)
    tok = jnp.repeat(jnp.arange(n), TOP_K)
    order = jnp.argsort(e_flat)
    e_s, tok_s, g_s = e_flat[order], tok[order], gates.reshape(-1)[order]
    counts = jnp.zeros((N_EXPERTS,), jnp.int32).at[e_flat].add(1)
    padded = (counts + MOE_BM - 1) // MOE_BM * MOE_BM
    start_unp = jnp.cumsum(counts) - counts
    ends_pad = jnp.cumsum(padded)
    start_pad = ends_pad - padded
    dest = start_pad[e_s] + (jnp.arange(a) - start_unp[e_s])
    rows = (-(-a // MOE_BM)) * MOE_BM + N_EXPERTS * MOE_BM
    nblk = rows // MOE_BM
    xs = jnp.zeros((rows, D), x.dtype).at[dest].set(xt[tok_s])
    blk_e = jnp.clip(jnp.searchsorted(ends_pad, jnp.arange(nblk) * MOE_BM, side='right'), 0, N_EXPERTS - 1)

    def expert_block(args):
        xb, e = args
        return swiglu(xb, w1[e], w3[e], w2[e])

    ys = lax.map(expert_block, (xs.reshape(nblk, MOE_BM, D), blk_e)).reshape(rows, D)
    out = jnp.zeros((n, D), x.dtype).at[tok_s].add(ys[dest] * g_s[:, None].astype(x.dtype))
    return out.reshape(B, L, D)


def gather_pages(pool, page_table):
    return pool[page_table].reshape(page_table.shape[0], -1, KV_HEADS, HEAD_DIM)


def mixer(hn, start, keep, hg0, cv0, pck, pcv, psk, psv, wbk, wbv,
          w_in, w_out, lower, hg_g, conv_w, q_g, k_g, cmp_w):
    B, L, _ = hn.shape
    (hq, hf, hi, hgt, cb, cc, ch, q, kc, vc, ks, vs, kw, vw, gl) = split_in(hn @ w_in)
    o_hg, s_hg = hgrn2(hq, hf, hi, hgt, lower, hg0, hg_g)
    o_cv, s_cv = short_conv(cb, cc, ch, cv0, conv_w)
    pos = start + jnp.arange(L)
    kvh = lambda t: t.reshape(B, L, KV_HEADS, HEAD_DIM)
    qn = rmsnorm(q.reshape(B, L, N_HEADS, HEAD_DIM), q_g)
    q_rot = rope(qn, pos)
    k_c = rmsnorm(kvh(kc), k_g[0])
    k_s = rope(rmsnorm(kvh(ks), k_g[1]), pos)
    k_w = rope(rmsnorm(kvh(kw), k_g[2]), pos)
    v_c, v_s, v_w = kvh(vc), kvh(vs), kvh(vw)
    full_ck = jnp.concatenate([pck.astype(k_c.dtype), k_c], axis=1)
    full_cv = jnp.concatenate([pcv.astype(v_c.dtype), v_c], axis=1)
    full_sk = jnp.concatenate([psk.astype(k_s.dtype), k_s], axis=1)
    full_sv = jnp.concatenate([psv.astype(v_s.dtype), v_s], axis=1)
    o_c, imp = compressed_branch(qn, full_ck, full_cv, cmp_w, pos)
    sel = select_blocks(imp, pos)
    o_s = selected_branch(q_rot, full_sk, full_sv, sel, pos)
    o_w, wk, wv = window_branch(q_rot, wbk, wbv, k_w, v_w, start, keep)
    gate = jax.nn.sigmoid(gl.astype(jnp.float32)).astype(hn.dtype).reshape(B, L, 3, N_HEADS, 1)
    o_att = gate[:, :, 0] * o_c + gate[:, :, 1] * o_s + gate[:, :, 2] * o_w
    y = jnp.concatenate([o_hg, o_cv, o_att.reshape(B, L, ATT_WIDTH)], axis=-1) @ w_out
    return y, (k_c, v_c, k_s, v_s, wk, wv, s_hg, s_cv)


def setup_inputs(seed: int = 0) -> dict:
    key = jax.random.key(seed)
    k = jax.random.split(key, 28)
    f32 = jnp.float32
    n_pages = PAST_LEN // PAGE_SIZE
    n_phys = (DEC_BATCH * n_pages * 5) // 4
    win_buf = min(WINDOW, PAST_LEN)

    def nrm(kk, shape, scale=1.0):
        return scale * jax.random.normal(kk, shape, f32)

    pool = (DEPTH, n_phys, PAGE_SIZE, KV_HEADS, HEAD_DIM)
    wbuf = (DEPTH, DEC_BATCH, win_buf, KV_HEADS, HEAD_DIM)
    page_table = jax.random.permutation(k[10], n_phys)[:DEC_BATCH * n_pages].reshape(DEC_BATCH, n_pages).astype(jnp.int32)
    return {
        'x_prompt': nrm(k[0], (BATCH, SEQ, D_MODEL)),
        'x_sample': nrm(k[1], (DEC_BATCH, DEC_SEQ, D_MODEL)),
        'cache_cmp_k': nrm(k[2], pool),
        'cache_cmp_v': nrm(k[3], pool),
        'cache_sel_k': nrm(k[4], pool),
        'cache_sel_v': nrm(k[5], pool),
        'cache_win_k': nrm(k[6], wbuf),
        'cache_win_v': nrm(k[7], wbuf),
        'state_hg': nrm(k[8], (DEPTH, DEC_BATCH, HG_HEADS, HG_DK, HG_DV), 0.5),
        'state_conv': nrm(k[9], (DEPTH, DEC_BATCH, CONV_W - 1, CONV_CH)),
        'page_table': page_table,
        'ln1_g': 1.0 + nrm(k[11], (DEPTH, D_MODEL), 0.02),
        'ln2_g': 1.0 + nrm(k[12], (DEPTH, D_MODEL), 0.02),
        'w_in': nrm(k[13], (DEPTH, D_MODEL, IN_COLS), D_MODEL ** -0.5),
        'w_out': nrm(k[14], (DEPTH, D_MIX, D_MODEL), D_MIX ** -0.5),
        'hg_lb': nrm(k[15], (DEPTH, HG_HEADS * HG_DK)),
        'hg_norm_g': 1.0 + nrm(k[16], (DEPTH, HG_DV), 0.02),
        'conv_w': nrm(k[17], (DEPTH, CONV_W, CONV_CH), CONV_W ** -0.5),
        'q_norm_g': 1.0 + nrm(k[18], (DEPTH, HEAD_DIM), 0.02),
        'k_norm_g': 1.0 + nrm(k[19], (DEPTH, 3, HEAD_DIM), 0.02),
        'cmp_pos_w': (1.0 + nrm(k[20], (DEPTH, 2, BLOCK), 0.1)) / BLOCK,
        'ffn_w1': nrm(k[21], (N_DENSE, D_MODEL, D_FF), D_MODEL ** -0.5),
        'ffn_w3': nrm(k[22], (N_DENSE, D_MODEL, D_FF), D_MODEL ** -0.5),
        'ffn_w2': nrm(k[23], (N_DENSE, D_FF, D_MODEL), D_FF ** -0.5),
        'router_w': nrm(k[24], (N_MOE, D_MODEL, N_EXPERTS), D_MODEL ** -0.5),
        'moe_w1': nrm(k[25], (N_MOE, N_EXPERTS, D_MODEL, D_FF), D_MODEL ** -0.5),
        'moe_w3': nrm(k[26], (N_MOE, N_EXPERTS, D_MODEL, D_FF), D_MODEL ** -0.5),
        'moe_w2': nrm(k[27], (N_MOE, N_EXPERTS, D_FF, D_MODEL), D_FF ** -0.5),
    }


def reference(x_prompt, x_sample, cache_cmp_k, cache_cmp_v, cache_sel_k, cache_sel_v,
              cache_win_k, cache_win_v, state_hg, state_conv, page_table,
              ln1_g, ln2_g, w_in, w_out, hg_lb, hg_norm_g, conv_w, q_norm_g, k_norm_g,
              cmp_pos_w, ffn_w1, ffn_w3, ffn_w2, router_w, moe_w1, moe_w3, moe_w2):
    keep = min(WINDOW, PAST_LEN)
    lb_sm = jax.nn.softmax(hg_lb.astype(jnp.float32), axis=0)
    lower = jnp.cumsum(lb_sm, axis=0) - lb_sm[0]

    def layer(x, l, start, hg0, cv0, pck, pcv, psk, psv, wbk, wbv):
        m, st = mixer(rmsnorm(x, ln1_g[l]), start, keep, hg0, cv0, pck, pcv, psk, psv, wbk, wbv,
                      w_in[l], w_out[l], lower[l], hg_norm_g[l], conv_w[l], q_norm_g[l], k_norm_g[l], cmp_pos_w[l])
        x = x + m
        hn = rmsnorm(x, ln2_g[l])
        if l % 2 == 0:
            f = swiglu(hn, ffn_w1[l // 2], ffn_w3[l // 2], ffn_w2[l // 2])
        else:
            f = moe_swiglu(hn, router_w[l // 2], moe_w1[l // 2], moe_w3[l // 2], moe_w2[l // 2])
        return x + f, st

    b, dt = x_prompt.shape[0], x_prompt.dtype
    empty = jnp.zeros((b, 0, KV_HEADS, HEAD_DIM), dt)
    hg_zero = jnp.zeros((b, HG_HEADS, HG_DK, HG_DV), dt)
    conv_zero = jnp.zeros((b, CONV_W - 1, CONV_CH), dt)
    yp, ys = x_prompt, x_sample
    sp, ss = [], []
    for l in range(DEPTH):
        yp, st_p = layer(yp, l, 0, hg_zero, conv_zero, empty, empty, empty, empty, empty, empty)
        ys, st_s = layer(ys, l, PAST_LEN, state_hg[l], state_conv[l],
                         gather_pages(cache_cmp_k[l], page_table), gather_pages(cache_cmp_v[l], page_table),
                         gather_pages(cache_sel_k[l], page_table), gather_pages(cache_sel_v[l], page_table),
                         cache_win_k[l], cache_win_v[l])
        sp.append(st_p)
        ss.append(st_s)

    def stk(states, i):
        return jnp.stack([s[i] for s in states])

    return (yp, ys,
            stk(sp, 0), stk(sp, 1), stk(sp, 2), stk(sp, 3), stk(sp, 4), stk(sp, 5), stk(sp, 6), stk(sp, 7),
            stk(ss, 0), stk(ss, 1), stk(ss, 2), stk(ss, 3), stk(ss, 4), stk(ss, 5), stk(ss, 6), stk(ss, 7))
```

```python
import functools
import math

import jax
import jax.numpy as jnp
import numpy as np
from jax import lax
from jax.experimental import pallas as pl
from jax.experimental.pallas import tpu as pltpu

F32 = jnp.float32
BF16 = jnp.bfloat16

D_MODEL = 2048
HG_HEADS = 4
HG_DK = 128
HG_DV = 128
HG_WIDTH = HG_HEADS * HG_DV
CONV_CH = 512
CONV_W = 3
N_HEADS = 8
HEAD_DIM = 128
KV_HEADS = 2
GQA = N_HEADS // KV_HEADS
ATT_WIDTH = N_HEADS * HEAD_DIM
KV_WIDTH = KV_HEADS * HEAD_DIM
BLOCK = 64
SEL_TOPK = 16
WINDOW = 512
ROPE_THETA = 10000.0
FORCE_SCORE = 1.0e4
NEG_BIG = -1.0e30
TINY = 1.0e-30
D_FF = 5632
N_EXPERTS = 8
TOP_K = 2
EPS = 1e-6
PAGE_SIZE = 128

OFF_HQ, OFF_HF, OFF_HI, OFF_HG = 0, 512, 1024, 1536
OFF_CB, OFF_CC, OFF_CH = 2048, 2560, 3072
OFF_Q = 3584
OFF_KC, OFF_VC, OFF_KS, OFF_VS, OFF_KW, OFF_VW = 4608, 4864, 5120, 5376, 5632, 5888
OFF_GATE = 6144
IN_MAIN = 6144
LANE = 128
SUB = 16
HG_CHUNK = 128
VMEM_LIMIT = 56 * 1024 * 1024
SCALE = HEAD_DIM ** -0.5


def _params(sem, vmem=VMEM_LIMIT):
    return pltpu.CompilerParams(dimension_semantics=sem, vmem_limit_bytes=vmem)


def _nt(a, b):
    return lax.dot_general(a, b, (((1,), (1,)), ((), ())), preferred_element_type=F32)


def _dot(a, b):
    return jnp.dot(a, b, preferred_element_type=F32)


def _nt_f32(a, b):
    return lax.dot_general(a, b, (((1,), (1,)), ((), ())), preferred_element_type=F32,
                           precision=lax.Precision.HIGHEST)


def _dot_f32(a, b):
    return jnp.dot(a, b, preferred_element_type=F32, precision=lax.Precision.HIGHEST)


def _sigmoid(x):
    return 1.0 / (1.0 + jnp.exp(-x))


def _norm_body(x, g):
    ms = jnp.mean(x * x, axis=-1, keepdims=True)
    return (x * lax.rsqrt(ms + EPS)) * g


def _norm_kernel(x_ref, g_ref, h_ref):
    h_ref[...] = _norm_body(x_ref[...], g_ref[...]).astype(h_ref.dtype)


def _add_norm_kernel(x_ref, d_ref, g_ref, xn_ref, h_ref):
    x = x_ref[...] + d_ref[...]
    xn_ref[...] = x
    h_ref[...] = _norm_body(x, g_ref[...]).astype(h_ref.dtype)


def _add_kernel(x_ref, d_ref, o_ref):
    o_ref[...] = x_ref[...] + d_ref[...]


def _row_tile(m, cap):
    t = min(m, cap)
    assert m % t == 0
    return t


def rmsnorm_rows(x, g, dtype=BF16):
    m, d = x.shape
    tm = _row_tile(m, 256)
    return pl.pallas_call(
        _norm_kernel,
        out_shape=jax.ShapeDtypeStruct((m, d), dtype),
        grid=(m // tm,),
        in_specs=[pl.BlockSpec((tm, d), lambda i: (i, 0)), pl.BlockSpec((1, d), lambda i: (0, 0))],
        out_specs=pl.BlockSpec((tm, d), lambda i: (i, 0)),
        compiler_params=_params(("arbitrary",)),
        name="rmsnorm",
    )(x, g.reshape(1, d))


def add_rmsnorm_rows(x, delta, g, dtype=BF16):
    m, d = x.shape
    tm = _row_tile(m, 256)
    row = pl.BlockSpec((tm, d), lambda i: (i, 0))
    return pl.pallas_call(
        _add_norm_kernel,
        out_shape=(jax.ShapeDtypeStruct((m, d), F32), jax.ShapeDtypeStruct((m, d), dtype)),
        grid=(m // tm,),
        in_specs=[row, row, pl.BlockSpec((1, d), lambda i: (0, 0))],
        out_specs=(row, row),
        compiler_params=_params(("arbitrary",)),
        name="add_rmsnorm",
    )(x, delta, g.reshape(1, d))


def add_rows(x, delta):
    m, d = x.shape
    tm = _row_tile(m, 256)
    row = pl.BlockSpec((tm, d), lambda i: (i, 0))
    return pl.pallas_call(
        _add_kernel,
        out_shape=jax.ShapeDtypeStruct((m, d), F32),
        grid=(m // tm,),
        in_specs=[row, row],
        out_specs=row,
        compiler_params=_params(("arbitrary",)),
        name="residual_add",
    )(x, delta)


def _mm_kernel(*refs, n_lhs):
    x_refs = refs[:n_lhs]
    w_ref, o_ref, wb_ref = refs[n_lhs:]

    @pl.when(pl.program_id(1) == 0)
    def _():
        wb_ref[...] = w_ref[...].astype(BF16)

    acc = None
    off = 0
    for x_ref in x_refs:
        k = x_ref.shape[1]
        part = _dot(x_ref[...], wb_ref[off:off + k, :])
        acc = part if acc is None else acc + part
        off += k
    o_ref[...] = acc


def _mm_f32_kernel(*refs, n_lhs):
    x_refs = refs[:n_lhs]
    w_ref, o_ref = refs[n_lhs:]
    acc = None
    off = 0
    for x_ref in x_refs:
        k = x_ref.shape[1]
        part = _dot_f32(x_ref[...], w_ref[off:off + k, :])
        acc = part if acc is None else acc + part
        off += k
    o_ref[...] = acc


def matmul_rows(xs, w, n_cols, tn, lead=(), tm_cap=1024):
    m = xs[0].shape[0]
    k = sum(x.shape[1] for x in xs)
    assert w.shape[-2] == k and n_cols % tn == 0 and w.ndim == 2 + len(lead)
    exact = xs[0].dtype == F32
    assert all(x.dtype == xs[0].dtype for x in xs)
    tm = _row_tile(m, tm_cap)
    in_specs = [pl.BlockSpec((tm, x.shape[1]), lambda j, i: (i, 0)) for x in xs]
    in_specs.append(pl.BlockSpec((None,) * len(lead) + (k, tn), lambda j, i: tuple(lead) + (0, j)))
    return pl.pallas_call(
        functools.partial(_mm_f32_kernel if exact else _mm_kernel, n_lhs=len(xs)),
        out_shape=jax.ShapeDtypeStruct((m, n_cols), F32),
        grid=(n_cols // tn, m // tm),
        in_specs=in_specs,
        out_specs=pl.BlockSpec((tm, tn), lambda j, i: (i, j)),
        scratch_shapes=[] if exact else [pltpu.VMEM((k, tn), BF16)],
        compiler_params=_params(("arbitrary", "arbitrary")),
        name="projection_f32" if exact else "projection",
    )(*xs, w)


def _swiglu_tile(x, w1, w3, w2):
    if x.dtype == F32:
        h1 = _dot_f32(x, w1)
        h3 = _dot_f32(x, w3)
        return _dot_f32((h1 * _sigmoid(h1)) * h3, w2)
    h1 = _dot(x, w1.astype(BF16))
    h3 = _dot(x, w3.astype(BF16))
    a = (h1 * _sigmoid(h1)) * h3
    return _dot(a.astype(BF16), w2.astype(BF16))


def _ffn_kernel(x_ref, w1_ref, w3_ref, w2_ref, o_ref):
    f = pl.program_id(1)
    y = _swiglu_tile(x_ref[...], w1_ref[...], w3_ref[...], w2_ref[...])

    @pl.when(f == 0)
    def _():
        o_ref[...] = y

    @pl.when(f > 0)
    def _():
        o_ref[...] += y


def ffn_rows(x, w1, w3, w2, lead=(), tf=256):
    m, d = x.shape
    dff = w1.shape[-1]
    tm = _row_tile(m, 1024)
    none = (None,) * len(lead)
    lead = tuple(lead)
    return pl.pallas_call(
        _ffn_kernel,
        out_shape=jax.ShapeDtypeStruct((m, d), F32),
        grid=(m // tm, dff // tf),
        in_specs=[
            pl.BlockSpec((tm, d), lambda i, f: (i, 0)),
            pl.BlockSpec(none + (d, tf), lambda i, f: lead + (0, f)),
            pl.BlockSpec(none + (d, tf), lambda i, f: lead + (0, f)),
            pl.BlockSpec(none + (tf, d), lambda i, f: lead + (f, 0)),
        ],
        out_specs=pl.BlockSpec((tm, d), lambda i, f: (i, 0)),
        compiler_params=_params(("arbitrary", "arbitrary")),
        name="swiglu_ffn",
    )(x, w1, w3, w2)


def _moe_kernel(te_ref, tv_ref, x_ref, g_ref, w1_ref, w3_ref, w2_ref, o_ref, *, sub):
    i = pl.program_id(0)
    f = pl.program_id(1)
    tm = x_ref.shape[0]
    valid = tv_ref[i]
    exact = x_ref.dtype == F32
    w1 = w1_ref[...] if exact else w1_ref[...].astype(BF16)
    w3 = w3_ref[...] if exact else w3_ref[...].astype(BF16)
    w2 = w2_ref[...] if exact else w2_ref[...].astype(BF16)
    for r in range(tm // sub):
        rows = slice(r * sub, (r + 1) * sub)

        @pl.when(valid > r * sub)
        def _():
            y = _swiglu_tile(x_ref[rows, :], w1, w3, w2)

            @pl.when(f == 0)
            def _():
                o_ref[rows, :] = y

            @pl.when(f > 0)
            def _():
                o_ref[rows, :] += y

            @pl.when(f == pl.num_programs(1) - 1)
            def _():
                o_ref[rows, :] = o_ref[rows, :] * g_ref[rows, :]

        @pl.when(valid <= r * sub)
        def _():
            @pl.when(f == 0)
            def _():
                o_ref[rows, :] = jnp.zeros((sub, o_ref.shape[1]), F32)


def moe_rows(xs, gate_rows, tile_expert, tile_valid, w1, w3, w2, layer, tm, sub, tf=256):
    rows, d = xs.shape
    dff = w1.shape[-1]
    n_tiles = rows // tm

    def fblk(i, f, tv):
        return jnp.where(tv[i] > 0, f, 0)

    grid_spec = pltpu.PrefetchScalarGridSpec(
        num_scalar_prefetch=2,
        grid=(n_tiles, dff // tf),
        in_specs=[
            pl.BlockSpec((tm, d), lambda i, f, te, tv: (i, 0)),
            pl.BlockSpec((tm, 1), lambda i, f, te, tv: (i, 0)),
            pl.BlockSpec((None, None, d, tf), lambda i, f, te, tv: (layer, te[i], 0, fblk(i, f, tv))),
            pl.BlockSpec((None, None, d, tf), lambda i, f, te, tv: (layer, te[i], 0, fblk(i, f, tv))),
            pl.BlockSpec((None, None, tf, d), lambda i, f, te, tv: (layer, te[i], fblk(i, f, tv), 0)),
        ],
        out_specs=pl.BlockSpec((tm, d), lambda i, f, te, tv: (i, 0)),
    )
    return pl.pallas_call(
        functools.partial(_moe_kernel, sub=sub),
        out_shape=jax.ShapeDtypeStruct((rows, d), F32),
        grid_spec=grid_spec,
        compiler_params=_params(("arbitrary", "arbitrary")),
        name="moe_swiglu",
    )(tile_expert, tile_valid, xs, gate_rows, w1, w3, w2)


def _router_kernel(x_ref, g_ref, r_ref, e_ref, p_ref):
    h = _norm_body(x_ref[...], g_ref[...])
    logits = jnp.dot(h, r_ref[...], preferred_element_type=F32, precision=lax.Precision.HIGHEST)
    lane = lax.broadcasted_iota(jnp.int32, logits.shape, 1)
    logits = jnp.where(lane < N_EXPERTS, logits, -jnp.inf)
    v1 = jnp.max(logits, axis=-1, keepdims=True)
    i1 = jnp.min(jnp.where(logits == v1, lane, LANE), axis=-1, keepdims=True)
    rest = jnp.where(lane == i1, -jnp.inf, logits)
    v2 = jnp.max(rest, axis=-1, keepdims=True)
    i2 = jnp.min(jnp.where(rest == v2, lane, LANE), axis=-1, keepdims=True)
    e2 = jnp.exp(v2 - v1)
    den = 1.0 + e2
    e_ref[...] = jnp.where(lane == 0, i1, jnp.where(lane == 1, i2, 0))
    p_ref[...] = jnp.where(lane == 0, 1.0 / den, jnp.where(lane == 1, e2 / den, 0.0))


def router_rows(x, g, router_w):
    m, d = x.shape
    tm = _row_tile(m, 256)
    rw = jnp.pad(router_w, ((0, 0), (0, LANE - router_w.shape[1])))
    row = pl.BlockSpec((tm, LANE), lambda i: (i, 0))
    e, p = pl.pallas_call(
        _router_kernel,
        out_shape=(jax.ShapeDtypeStruct((m, LANE), jnp.int32), jax.ShapeDtypeStruct((m, LANE), F32)),
        grid=(m // tm,),
        in_specs=[pl.BlockSpec((tm, d), lambda i: (i, 0)), pl.BlockSpec((1, d), lambda i: (0, 0)),
                  pl.BlockSpec((d, LANE), lambda i: (0, 0))],
        out_specs=(row, row),
        compiler_params=_params(("arbitrary",)),
        name="router_top2",
    )(x, g.reshape(1, d), rw)
    return e[:, :TOP_K], p[:, :TOP_K]


def moe_ffn(x, hn, ln_g, router_w, w1, w3, w2, layer, tm, sub):
    n, d = hn.shape
    top_e, gates = router_rows(x, ln_g, router_w)
    a = n * TOP_K
    e_flat = top_e.reshape(-1)
    onehot = (e_flat[:, None] == jnp.arange(N_EXPERTS)[None, :]).astype(jnp.int32)
    within = jnp.cumsum(onehot, axis=0) - onehot
    counts = jnp.sum(onehot, axis=0)
    padded = (counts + tm - 1) // tm * tm
    ends_pad = jnp.cumsum(padded)
    start_pad = ends_pad - padded
    dest = start_pad[e_flat] + jnp.sum(within * onehot, axis=1)
    rows = (-(-a // tm)) * tm + N_EXPERTS * tm
    n_tiles = rows // tm
    tile_start = jnp.arange(n_tiles, dtype=jnp.int32) * tm
    tile_expert = jnp.clip(jnp.searchsorted(ends_pad, tile_start, side='right'), 0, N_EXPERTS - 1).astype(jnp.int32)
    tile_valid = jnp.clip(start_pad[tile_expert] + counts[tile_expert] - tile_start, 0, tm).astype(jnp.int32)
    tok = jnp.arange(a, dtype=jnp.int32) // TOP_K
    src = jnp.zeros((rows,), jnp.int32).at[dest].set(tok)
    gate_rows = jnp.zeros((rows,), F32).at[dest].set(gates.reshape(-1))
    xs = jnp.take(hn, src, axis=0)
    ys = moe_rows(xs, gate_rows.reshape(rows, 1), tile_expert, tile_valid, w1, w3, w2, layer, tm, sub)
    back = jnp.take(ys, dest, axis=0).reshape(n, TOP_K, d)
    return back[:, 0] + back[:, 1]


def _hgrn_kernel(zq_ref, zf_ref, zi_ref, zg_ref, lb_ref, ng_ref, tri_ref, o_ref, s_ref,
                 st_ref, q_s, b_s, bx_s, k_s, v_s, a_s, od_s):
    c = pl.program_id(2)
    n = HG_CHUNK

    @pl.when(c == 0)
    def _():
        st_ref[...] = jnp.zeros_like(st_ref)

    lb = lb_ref[...]
    zf = zf_ref[...]
    sg = _sigmoid(zf)
    f = lb + (1.0 - lb) * sg
    lf = jnp.log(jnp.maximum(f, TINY))
    k = (1.0 - lb) * _sigmoid(-zf)
    b = jnp.dot(tri_ref[...], lf, preferred_element_type=F32, precision=lax.Precision.HIGHEST)
    q = zq_ref[...]
    v = zi_ref[...]
    q_s[...] = q
    b_s[...] = b
    bx_s[...] = b - lf
    k_s[...] = k
    v_s[...] = v
    lane = lax.broadcasted_iota(jnp.int32, (SUB, n), 1)
    row = lax.broadcasted_iota(jnp.int32, (SUB, 1), 0)

    def sub_chunk(i, carry):
        r0 = pl.multiple_of(i * SUB, SUB)
        rows = pl.ds(r0, SUB)
        q_i = q_s[rows, :]
        b_i = b_s[rows, :]
        k_i = k_s[rows, :]
        v_i = v_s[rows, :]
        ref = bx_s[pl.ds(r0, 1), :]
        qs = q_i * jnp.exp(b_i - ref)
        ks = k_s[...] * jnp.exp(jnp.minimum(ref - b_s[...], 0.0))
        a_off = _nt(qs.astype(BF16), ks.astype(BF16))
        a_s[rows, :] = jnp.where(lane < r0, a_off, 0.0)
        od = jnp.zeros((SUB, n), F32)
        for s in range(SUB):
            d = q_i * jnp.exp(jnp.minimum(b_i - b_i[s:s + 1, :], 0.0)) * k_i[s:s + 1, :]
            a = jnp.sum(d, axis=-1, keepdims=True)
            od = od + jnp.where(row >= s, a, 0.0) * v_i[s:s + 1, :]
        od_s[rows, :] = od
        return carry

    lax.fori_loop(0, n // SUB, sub_chunk, 0)

    st = st_ref[...]
    vb = v.astype(BF16)
    o = _dot(a_s[...].astype(BF16), vb) + od_s[...] + _nt((q * jnp.exp(b)).astype(BF16), st.astype(BF16))
    ms = jnp.mean(o * o, axis=-1, keepdims=True)
    g = zg_ref[...]
    o = (o * lax.rsqrt(ms + EPS)) * ng_ref[...] * (g * _sigmoid(g))
    o_ref[...] = o.astype(o_ref.dtype)

    bl = b[n - 1:n, :]
    kd = k * jnp.exp(bl - b)
    st_new = jnp.exp(bl) * st + _dot(v.T.astype(BF16), kd.astype(BF16))
    st_ref[...] = st_new

    @pl.when(c == pl.num_programs(2) - 1)
    def _():
        s_ref[...] = st_new.T


def hgrn_prompt(z, lower, norm_g, batch, seq):
    n = HG_CHUNK
    nc = seq // n
    cpb = seq // n
    tri = jnp.tril(jnp.ones((n, n), F32))

    def col(base):
        return pl.BlockSpec((n, LANE), lambda b, h, c, base=base: (b * cpb + c, base // LANE + h))

    vec = pl.BlockSpec((1, LANE), lambda b, h, c: (0, h))
    return pl.pallas_call(
        _hgrn_kernel,
        out_shape=(jax.ShapeDtypeStruct((batch * seq, HG_WIDTH), BF16),
                   jax.ShapeDtypeStruct((batch, HG_HEADS, HG_DK, HG_DV), F32)),
        grid=(batch, HG_HEADS, nc),
        in_specs=[col(OFF_HQ), col(OFF_HF), col(OFF_HI), col(OFF_HG), vec,
                  pl.BlockSpec((1, LANE), lambda b, h, c: (0, 0)),
                  pl.BlockSpec((n, n), lambda b, h, c: (0, 0))],
        out_specs=(pl.BlockSpec((n, LANE), lambda b, h, c: (b * cpb + c, h)),
                   pl.BlockSpec((None, None, HG_DK, HG_DV), lambda b, h, c: (b, h, 0, 0))),
        scratch_shapes=[pltpu.VMEM((HG_DV, HG_DK), F32)] + [pltpu.VMEM((n, LANE), F32)] * 7,
        compiler_params=_params(("arbitrary", "arbitrary", "arbitrary")),
        name="hgrn2_prompt",
    )(z, z, z, z, lower.reshape(1, HG_WIDTH), norm_g.reshape(1, HG_DV), tri)


CONV_PAD = 8


def _conv_kernel(cb_ref, cc_ref, ch_ref, w_ref, o_ref, s_ref, u_s):
    seq = cb_ref.shape[0]
    u = cc_ref[...] * ch_ref[...]
    u_s[0:CONV_PAD, :] = jnp.zeros((CONV_PAD, LANE), F32)
    u_s[CONV_PAD:CONV_PAD + seq, :] = u
    w = w_ref[...]
    y = (w[0:1, :] * u_s[CONV_PAD - 2:CONV_PAD - 2 + seq, :]
         + w[1:2, :] * u_s[CONV_PAD - 1:CONV_PAD - 1 + seq, :]
         + w[2:3, :] * u)
    o_ref[...] = (cb_ref[...] * y).astype(o_ref.dtype)
    s_ref[...] = u_s[CONV_PAD + seq - (CONV_W - 1):CONV_PAD + seq, :]


def conv_prompt(z, conv_w, batch, seq):
    nblk = CONV_CH // LANE

    def col(base):
        return pl.BlockSpec((seq, LANE), lambda b, j, base=base: (b, base // LANE + j))

    return pl.pallas_call(
        _conv_kernel,
        out_shape=(jax.ShapeDtypeStruct((batch * seq, CONV_CH), BF16),
                   jax.ShapeDtypeStruct((batch, CONV_W - 1, CONV_CH), F32)),
        grid=(batch, nblk),
        in_specs=[col(OFF_CB), col(OFF_CC), col(OFF_CH), pl.BlockSpec((CONV_W, LANE), lambda b, j: (0, j))],
        out_specs=(pl.BlockSpec((seq, LANE), lambda b, j: (b, j)),
                   pl.BlockSpec((None, CONV_W - 1, LANE), lambda b, j: (b, 0, j))),
        scratch_shapes=[pltpu.VMEM((CONV_PAD + seq, LANE), F32)],
        compiler_params=_params(("arbitrary", "arbitrary")),
        name="short_conv_prompt",
    )(z, z, z, conv_w)


def _head_norm(x, g):
    ms = jnp.mean(x * x, axis=-1, keepdims=True)
    return (x * lax.rsqrt(ms + EPS)) * g


def _rope(x, cos2, sin2):
    return x * cos2 + pltpu.roll(x, HEAD_DIM // 2, 1) * sin2


def _qkprep_kernel(qa_ref, qb_ref, kc_ref, ks_ref, kw_ref, qg_ref, kg_ref, cos_ref, sin_ref,
                   qn_ref, qr_ref, okc_ref, oks_ref, okw_ref):
    cos2 = cos_ref[...]
    sin2 = sin_ref[...]
    qg = qg_ref[...]
    half = ATT_WIDTH // 2
    for h in range(N_HEADS):
        src = qa_ref if h < N_HEADS // 2 else qb_ref
        lo = (h * HEAD_DIM) % half
        xn = _head_norm(src[:, lo:lo + HEAD_DIM], qg)
        qn_ref[:, h * HEAD_DIM:(h + 1) * HEAD_DIM] = xn.astype(qn_ref.dtype)
        qr_ref[:, h * HEAD_DIM:(h + 1) * HEAD_DIM] = _rope(xn, cos2, sin2).astype(qr_ref.dtype)
    for h in range(KV_HEADS):
        sl = slice(h * HEAD_DIM, (h + 1) * HEAD_DIM)
        okc_ref[:, sl] = _head_norm(kc_ref[:, sl], kg_ref[0:1, :])
        oks_ref[:, sl] = _rope(_head_norm(ks_ref[:, sl], kg_ref[1:2, :]), cos2, sin2)
        okw_ref[:, sl] = _rope(_head_norm(kw_ref[:, sl], kg_ref[2:3, :]), cos2, sin2)


def rope_tables(pos):
    half = HEAD_DIM // 2
    inv = jnp.exp(-math.log(ROPE_THETA) * jnp.arange(half, dtype=F32) / half)
    ang = pos.astype(F32)[:, None] * inv[None, :]
    cos, sin = jnp.cos(ang), jnp.sin(ang)
    return jnp.concatenate([cos, cos], axis=1), jnp.concatenate([-sin, sin], axis=1)


def qk_prep(z, q_g, k_g, cos2, sin2, tm, q_dtype=BF16):
    m = z.shape[0]
    period_blocks = cos2.shape[0] // tm
    half = ATT_WIDTH // 2

    def col(base, width):
        return pl.BlockSpec((tm, width), lambda i, base=base, width=width: (i, base // width))

    tab = pl.BlockSpec((tm, LANE), lambda i: (i % period_blocks, 0))
    return pl.pallas_call(
        _qkprep_kernel,
        out_shape=(jax.ShapeDtypeStruct((m, ATT_WIDTH), q_dtype), jax.ShapeDtypeStruct((m, ATT_WIDTH), q_dtype),
                   jax.ShapeDtypeStruct((m, KV_WIDTH), F32), jax.ShapeDtypeStruct((m, KV_WIDTH), F32),
                   jax.ShapeDtypeStruct((m, KV_WIDTH), F32)),
        grid=(m // tm,),
        in_specs=[col(OFF_Q, half), col(OFF_Q + half, half), col(OFF_KC, KV_WIDTH), col(OFF_KS, KV_WIDTH),
                  col(OFF_KW, KV_WIDTH), pl.BlockSpec((1, LANE), lambda i: (0, 0)),
                  pl.BlockSpec((3, LANE), lambda i: (0, 0)), tab, tab],
        out_specs=(pl.BlockSpec((tm, ATT_WIDTH), lambda i: (i, 0)), pl.BlockSpec((tm, ATT_WIDTH), lambda i: (i, 0)),
                   pl.BlockSpec((tm, KV_WIDTH), lambda i: (i, 0)), pl.BlockSpec((tm, KV_WIDTH), lambda i: (i, 0)),
                   pl.BlockSpec((tm, KV_WIDTH), lambda i: (i, 0))),
        compiler_params=_params(("arbitrary",)),
        name="qk_prep",
    )(z, z, z, z, z, q_g.reshape(1, HEAD_DIM), k_g, cos2, sin2)


NBP = LANE
TQ = 256


def _softmax_masked(s, mask):
    s = jnp.where(mask, s, NEG_BIG)
    m = jnp.max(s, axis=-1, keepdims=True)
    e = jnp.where(mask, jnp.exp(s - m), 0.0)
    return e / jnp.maximum(jnp.sum(e, axis=-1, keepdims=True), TINY)


def _topk_mask(score, blk, n_cand, k):
    rank = jnp.zeros(score.shape, F32)
    for j in range(n_cand):
        sj = score[:, j:j + 1]
        beats = (sj > score) | ((sj == score) & (blk > j))
        rank = rank + jnp.where(beats, 1.0, 0.0)
    return (rank < k) & (blk < n_cand)


def _attn_prompt_kernel(qn_ref, qr_ref, kc_ref, vc_ref, ks_ref, vs_ref, kw_ref, vw_ref, gl_ref, wk_ref, wv_ref,
                        ex_ref, o_ref, ck_s, cv_s, ks_s, vs_s, kw_s, vw_s, *, nb):
    qi = pl.program_id(2)
    tq = qn_ref.shape[0]
    seq = kc_ref.shape[0]

    @pl.when(qi == 0)
    def _():
        ck_s[...] = _dot(wk_ref[...].astype(BF16), kc_ref[...].astype(BF16)).astype(BF16)
        cv_s[...] = _dot(wv_ref[...].astype(BF16), vc_ref[...].astype(BF16)).astype(BF16)
        ks_s[...] = ks_ref[...].astype(BF16)
        vs_s[...] = vs_ref[...].astype(BF16)
        kw_s[...] = kw_ref[...].astype(BF16)
        vw_s[...] = vw_ref[...].astype(BF16)

    pos = qi * tq + lax.broadcasted_iota(jnp.int32, (tq, 1), 0)
    blk = lax.broadcasted_iota(jnp.int32, (tq, NBP), 1)
    cmask = (blk * BLOCK + (BLOCK - 1) <= pos) & (blk < nb)
    ck = ck_s[...]
    cv = cv_s[...]
    imp = jnp.zeros((tq, NBP), F32)
    o_cmp = []
    for g in range(GQA):
        q = qn_ref[:, g * HEAD_DIM:(g + 1) * HEAD_DIM]
        p = _softmax_masked(_nt(q, ck) * SCALE, cmask)
        imp = imp + p
        o_cmp.append(_dot(p.astype(BF16), cv))

    cur = lax.shift_right_logical(pos, 6)
    score = jnp.where((blk == cur) | (blk == 0), FORCE_SCORE, jnp.where(blk <= cur, imp, -1.0))
    sel = _topk_mask(score, blk, nb, min(SEL_TOPK, nb))
    selx = _dot(jnp.where(sel, 1.0, 0.0).astype(BF16), ex_ref[...])
    kpos = lax.broadcasted_iota(jnp.int32, (tq, seq), 1)
    causal = kpos <= pos
    smask = causal & (selx > 0.5)
    wmask = causal & (kpos > pos - WINDOW)
    gates = _sigmoid(gl_ref[...])
    ks = ks_s[...]
    vs = vs_s[...]
    kw = kw_s[...]
    vw = vw_s[...]
    for g in range(GQA):
        q = qr_ref[:, g * HEAD_DIM:(g + 1) * HEAD_DIM]
        o_sel = _dot(_softmax_masked(_nt(q, ks) * SCALE, smask).astype(BF16), vs)
        o_win = _dot(_softmax_masked(_nt(q, kw) * SCALE, wmask).astype(BF16), vw)
        o = (gates[:, g:g + 1] * o_cmp[g] + gates[:, GQA + g:GQA + g + 1] * o_sel
             + gates[:, 2 * GQA + g:2 * GQA + g + 1] * o_win)
        o_ref[:, g * HEAD_DIM:(g + 1) * HEAD_DIM] = o.astype(o_ref.dtype)


def _arrange_gates(gl, m):
    g = gl[:, :3 * N_HEADS].reshape(m, 3, KV_HEADS, GQA).transpose(2, 0, 1, 3).reshape(KV_HEADS, m, 3 * GQA)
    return jnp.pad(g, ((0, 0), (0, 0), (0, LANE - 3 * GQA)))


def _block_pos_matrix(w, seq):
    nb = seq // BLOCK
    m = jnp.kron(jnp.eye(nb, dtype=F32), w[None, :])
    return jnp.pad(m, ((0, NBP - nb), (0, 0)))


def attn_prompt(qn, qr, kc, ks, kw, z, gl, cmp_w, batch, seq):
    m = batch * seq
    nb = seq // BLOCK
    assert seq % BLOCK == 0 and nb <= NBP
    tq = min(TQ, seq)
    qt = seq // tq
    half = ATT_WIDTH // KV_HEADS
    gates = _arrange_gates(gl, m)
    wk = _block_pos_matrix(cmp_w[0], seq)
    wv = _block_pos_matrix(cmp_w[1], seq)
    expand = _block_pos_matrix(jnp.ones((BLOCK,), F32), seq).astype(BF16)

    qspec = pl.BlockSpec((tq, half), lambda b, h, i: (b * qt + i, h))
    kspec = pl.BlockSpec((seq, HEAD_DIM), lambda b, h, i: (b, h))

    def vspec(base):
        return pl.BlockSpec((seq, HEAD_DIM), lambda b, h, i, base=base: (b, base // HEAD_DIM + h))

    full = pl.BlockSpec((NBP, seq), lambda b, h, i: (0, 0))
    return pl.pallas_call(
        functools.partial(_attn_prompt_kernel, nb=nb),
        out_shape=jax.ShapeDtypeStruct((m, ATT_WIDTH), BF16),
        grid=(batch, KV_HEADS, qt),
        in_specs=[qspec, qspec, kspec, vspec(OFF_VC), kspec, vspec(OFF_VS), kspec, vspec(OFF_VW),
                  pl.BlockSpec((None, tq, LANE), lambda b, h, i: (h, b * qt + i, 0)), full, full, full],
        out_specs=qspec,
        scratch_shapes=[pltpu.VMEM((NBP, HEAD_DIM), BF16)] * 2 + [pltpu.VMEM((seq, HEAD_DIM), BF16)] * 4,
        compiler_params=_params(("arbitrary", "arbitrary", "arbitrary")),
        name="nsa_prompt",
    )(qn, qr, kc, z, ks, z, kw, z, gates, wk, wv, expand)


def _gate_weights(w_in, layer):
    return jnp.pad(w_in[layer, :, OFF_GATE:], ((0, 0), (0, LANE - 3 * N_HEADS)))


def _kv4(t, batch, seq):
    return t.reshape(batch, seq, KV_HEADS, HEAD_DIM)


def prompt_mixer(hn, layer, batch, seq, keep, w_in, w_out, lower, hg_norm_g, conv_w, q_norm_g, k_norm_g,
                 cmp_pos_w):
    z = matmul_rows([hn], w_in, IN_MAIN, 512, lead=(layer,))
    gl = matmul_rows([hn], _gate_weights(w_in, layer), LANE, LANE)
    o_hg, s_hg = hgrn_prompt(z, lower[layer], hg_norm_g[layer], batch, seq)
    o_cv, s_cv = conv_prompt(z, conv_w[layer], batch, seq)
    cos2, sin2 = rope_tables(jnp.arange(seq))
    qn, qr, k_c, k_s, k_w = qk_prep(z, q_norm_g[layer], k_norm_g[layer], cos2, sin2, min(256, seq))
    o_att = attn_prompt(qn, qr, k_c, k_s, k_w, z, gl, cmp_pos_w[layer], batch, seq)
    y = matmul_rows([o_hg, o_cv, o_att], w_out, D_MODEL, 512, lead=(layer,))
    v_c = z[:, OFF_VC:OFF_VC + KV_WIDTH]
    v_s = z[:, OFF_VS:OFF_VS + KV_WIDTH]
    v_w = z[:, OFF_VW:OFF_VW + KV_WIDTH]
    states = (_kv4(k_c, batch, seq), _kv4(v_c, batch, seq), _kv4(k_s, batch, seq), _kv4(v_s, batch, seq),
              _kv4(k_w, batch, seq)[:, seq - keep:], _kv4(v_w, batch, seq)[:, seq - keep:], s_hg, s_cv)
    return y, states


def _recurrent_step_kernel(zrow_ref, qcol_ref, fcol_ref, lbcol_ref, ng_ref, cw_ref, s_ref, cs_ref,
                           ohg_ref, ocv_ref, so_ref, cso_ref):
    for h in range(HG_HEADS):
        lo = h * HG_DV
        zf = fcol_ref[h]
        lb = lbcol_ref[h]
        f = lb + (1.0 - lb) * _sigmoid(zf)
        decay = jnp.maximum(f, TINY)
        k = (1.0 - lb) * _sigmoid(-zf)
        v = zrow_ref[:, OFF_HI + lo:OFF_HI + lo + HG_DV]
        s_new = decay * s_ref[h] + k * v
        so_ref[h] = s_new
        o = jnp.sum(qcol_ref[h] * s_new, axis=0, keepdims=True)
        g = zrow_ref[:, OFF_HG + lo:OFF_HG + lo + HG_DV]
        ms = jnp.mean(o * o, axis=-1, keepdims=True)
        ohg_ref[:, lo:lo + HG_DV] = (o * lax.rsqrt(ms + EPS)) * ng_ref[...] * (g * _sigmoid(g))
    cb = zrow_ref[:, OFF_CB:OFF_CB + CONV_CH]
    u = zrow_ref[:, OFF_CC:OFF_CC + CONV_CH] * zrow_ref[:, OFF_CH:OFF_CH + CONV_CH]
    w = cw_ref[...]
    st = cs_ref[...]
    ocv_ref[...] = cb * (w[0:1, :] * st[0:1, :] + w[1:2, :] * st[1:2, :] + w[2:3, :] * u)
    cso_ref[0:1, :] = st[1:2, :]
    cso_ref[1:2, :] = u


def recurrent_step(z, layer, state_hg, state_conv, lower, norm_g, conv_w):
    nseq = z.shape[0]
    z3 = z.reshape(nseq, 1, IN_MAIN)
    qcol = z[:, OFF_HQ:OFF_HQ + HG_WIDTH].reshape(nseq, HG_HEADS, HG_DK, 1)
    fcol = z[:, OFF_HF:OFF_HF + HG_WIDTH].reshape(nseq, HG_HEADS, HG_DK, 1)
    colspec = pl.BlockSpec((None, HG_HEADS, HG_DK, 1), lambda b: (b, 0, 0, 0))
    rowspec = pl.BlockSpec((None, 1, HG_WIDTH), lambda b: (b, 0, 0))
    hg_spec_in = pl.BlockSpec((None, None, HG_HEADS, HG_DK, HG_DV), lambda b: (layer, b, 0, 0, 0))
    cs_spec_in = pl.BlockSpec((None, None, CONV_W - 1, CONV_CH), lambda b: (layer, b, 0, 0))
    ohg, ocv, s_hg, s_cv = pl.pallas_call(
        _recurrent_step_kernel,
        out_shape=(jax.ShapeDtypeStruct((nseq, 1, HG_WIDTH), F32), jax.ShapeDtypeStruct((nseq, 1, CONV_CH), F32),
                   jax.ShapeDtypeStruct((nseq, HG_HEADS, HG_DK, HG_DV), F32),
                   jax.ShapeDtypeStruct((nseq, CONV_W - 1, CONV_CH), F32)),
        grid=(nseq,),
        in_specs=[pl.BlockSpec((None, 1, IN_MAIN), lambda b: (b, 0, 0)), colspec, colspec,
                  pl.BlockSpec((HG_HEADS, HG_DK, 1), lambda b: (0, 0, 0)),
                  pl.BlockSpec((1, HG_DV), lambda b: (0, 0)),
                  pl.BlockSpec((CONV_W, CONV_CH), lambda b: (0, 0)), hg_spec_in, cs_spec_in],
        out_specs=(rowspec, rowspec, pl.BlockSpec((None, HG_HEADS, HG_DK, HG_DV), lambda b: (b, 0, 0, 0)),
                   pl.BlockSpec((None, CONV_W - 1, CONV_CH), lambda b: (b, 0, 0))),
        compiler_params=_params(("arbitrary",)),
        name="recurrent_step",
    )(z3, qcol, fcol, lower.reshape(HG_HEADS, HG_DK, 1), norm_g.reshape(1, HG_DV), conv_w, state_hg, state_conv)
    return ohg.reshape(nseq, HG_WIDTH), ocv.reshape(nseq, CONV_CH), s_hg, s_cv


PAGES_PER_STEP = 8
BLOCKS_PER_PAGE = PAGE_SIZE // BLOCK


def _compress_pages_kernel(pt_ref, *refs):
    k_refs = refs[:PAGES_PER_STEP]
    v_refs = refs[PAGES_PER_STEP:2 * PAGES_PER_STEP]
    wk_ref, wv_ref, ck_ref, cv_ref = refs[2 * PAGES_PER_STEP:]
    for page_refs, w_ref, o_ref in ((k_refs, wk_ref, ck_ref), (v_refs, wv_ref, cv_ref)):
        w = w_ref[...]
        for n, p_ref in enumerate(page_refs):
            x = p_ref[...] * w
            for j in range(BLOCKS_PER_PAGE):
                r = n * BLOCKS_PER_PAGE + j
                o_ref[r:r + 1, :] = jnp.sum(x[j * BLOCK:(j + 1) * BLOCK, :], axis=0, keepdims=True)


def compress_pages(pool_k, pool_v, page_table, layer, cmp_w):
    nseq, n_pages = page_table.shape
    assert n_pages % PAGES_PER_STEP == 0
    steps = n_pages // PAGES_PER_STEP
    out_rows = PAGES_PER_STEP * BLOCKS_PER_PAGE

    def page_spec(j):
        return pl.BlockSpec((None, None, PAGE_SIZE, KV_WIDTH),
                            lambda b, p, pt, j=j: (layer, pt[b * n_pages + p * PAGES_PER_STEP + j], 0, 0))

    wcol = pl.BlockSpec((PAGE_SIZE, 1), lambda b, p, pt: (0, 0))
    ospec = pl.BlockSpec((None, out_rows, KV_WIDTH), lambda b, p, pt: (b, p, 0))
    grid_spec = pltpu.PrefetchScalarGridSpec(
        num_scalar_prefetch=1,
        grid=(nseq, steps),
        in_specs=[page_spec(j) for j in range(PAGES_PER_STEP)] * 2 + [wcol, wcol],
        out_specs=(ospec, ospec),
    )
    n_blocks = n_pages * BLOCKS_PER_PAGE
    shape = jax.ShapeDtypeStruct((nseq, n_blocks, KV_WIDTH), F32)
    wk = jnp.tile(cmp_w[0], BLOCKS_PER_PAGE).reshape(PAGE_SIZE, 1)
    wv = jnp.tile(cmp_w[1], BLOCKS_PER_PAGE).reshape(PAGE_SIZE, 1)
    return pl.pallas_call(
        _compress_pages_kernel,
        out_shape=(shape, shape),
        grid_spec=grid_spec,
        compiler_params=_params(("arbitrary", "arbitrary")),
        name="compress_pages",
    )(page_table.reshape(-1), *([pool_k] * PAGES_PER_STEP), *([pool_v] * PAGES_PER_STEP), wk, wv)


QROWS = 8


def _rows_to_tile(rows):
    n = rows[0].shape[1]
    r = lax.broadcasted_iota(jnp.int32, (QROWS, n), 0)
    tile = jnp.zeros((QROWS, n), F32)
    for i, row in enumerate(rows):
        tile = jnp.where(r == i, jnp.broadcast_to(row, (QROWS, n)), tile)
    return tile


def _group_queries(q_ref, h):
    rows = [q_ref[:, (h * GQA + g) * HEAD_DIM:(h * GQA + g + 1) * HEAD_DIM] for g in range(GQA)]
    return _rows_to_tile(rows)


def _decode_select_kernel(qn_ref, ck_ref, cv_ref, oc_ref, idx_ref, *, n_blocks):
    assert n_blocks == NBP
    lane = lax.broadcasted_iota(jnp.int32, (1, NBP), 1)
    sub = lax.broadcasted_iota(jnp.int32, (NBP, NBP), 0)
    lane2 = lax.broadcasted_iota(jnp.int32, (NBP, NBP), 1)
    grow = lax.broadcasted_iota(jnp.int32, (QROWS, 1), 0)
    slot = lax.broadcasted_iota(jnp.int32, (SEL_TOPK, NBP), 0)
    for h in range(KV_HEADS):
        q = _group_queries(qn_ref, h)
        ck = ck_ref[:, h * HEAD_DIM:(h + 1) * HEAD_DIM]
        cv = cv_ref[:, h * HEAD_DIM:(h + 1) * HEAD_DIM]
        s = _nt_f32(q, ck) * SCALE
        m = jnp.max(s, axis=-1, keepdims=True)
        e = jnp.exp(s - m)
        p = e / jnp.maximum(jnp.sum(e, axis=-1, keepdims=True), TINY)
        o = _dot_f32(p, cv)
        for g in range(GQA):
            oc_ref[:, (h * GQA + g) * HEAD_DIM:(h * GQA + g + 1) * HEAD_DIM] = o[g:g + 1, :]
        imp = jnp.sum(jnp.where(grow < GQA, p, 0.0), axis=0, keepdims=True)
        imp_c = jnp.broadcast_to(imp, (NBP, NBP)).T
        imp_r = jnp.broadcast_to(imp, (NBP, NBP))
        beats = ((imp_c > imp_r) | ((imp_c == imp_r) & (sub < lane2))) & (sub > 0)
        rank = jnp.sum(jnp.where(beats, 1.0, 0.0), axis=0, keepdims=True)
        full_rank = jnp.where(lane == 0, 0.0, rank + 2.0)
        hit = full_rank.astype(jnp.int32) == slot
        idx = jnp.sum(jnp.where(hit, lane, 0), axis=-1, keepdims=True)
        srow = lax.broadcasted_iota(jnp.int32, (SEL_TOPK, 1), 0)
        idx_ref[h] = jnp.where(srow == 1, n_blocks, idx)


def decode_select(qn, ck, cv):
    nseq, n_blocks, _ = ck.shape
    qspec = pl.BlockSpec((None, 1, ATT_WIDTH), lambda b: (b, 0, 0))
    cspec = pl.BlockSpec((None, n_blocks, KV_WIDTH), lambda b: (b, 0, 0))
    oc, idx = pl.pallas_call(
        functools.partial(_decode_select_kernel, n_blocks=n_blocks),
        out_shape=(jax.ShapeDtypeStruct((nseq, 1, ATT_WIDTH), F32),
                   jax.ShapeDtypeStruct((nseq, KV_HEADS, SEL_TOPK, 1), jnp.int32)),
        grid=(nseq,),
        in_specs=[qspec, cspec, cspec],
        out_specs=(qspec, pl.BlockSpec((None, KV_HEADS, SEL_TOPK, 1), lambda b: (b, 0, 0, 0))),
        compiler_params=_params(("arbitrary",)),
        name="decode_select",
    )(qn.reshape(nseq, 1, ATT_WIDTH), ck, cv)
    return oc.reshape(nseq, ATT_WIDTH), idx.reshape(nseq, KV_HEADS, SEL_TOPK)


CACHED_SLOTS = tuple(j for j in range(SEL_TOPK) if j != 1)


SLOTS_PER_STEP = 5
SEL_ROWS = 1024
WIN_ROWS = WINDOW + LANE


def _gather_blocks_kernel(pt_ref, idx_ref, *refs):
    k_refs = refs[:SLOTS_PER_STEP]
    v_refs = refs[SLOTS_PER_STEP:2 * SLOTS_PER_STEP]
    ko_ref, vo_ref = refs[2 * SLOTS_PER_STEP:]
    for n in range(SLOTS_PER_STEP):
        ko_ref[n * BLOCK:(n + 1) * BLOCK, :] = k_refs[n][...]
        vo_ref[n * BLOCK:(n + 1) * BLOCK, :] = v_refs[n][...]


def gather_selected(sel_idx, page_table, pool_k, pool_v, layer):
    nseq, n_pages = page_table.shape
    n_cached = n_pages * BLOCKS_PER_PAGE
    ns = len(CACHED_SLOTS)
    assert ns % SLOTS_PER_STEP == 0 and BLOCKS_PER_PAGE == 2
    steps = ns // SLOTS_PER_STEP

    def blk_spec(n):
        def index(b, h, s, pt, idx):
            slot = s * SLOTS_PER_STEP + n
            slot = slot + jnp.where(slot >= 1, 1, 0)
            blk = jnp.clip(idx[(b * KV_HEADS + h) * SEL_TOPK + slot], 0, n_cached - 1)
            page = pt[b * n_pages + lax.shift_right_logical(blk, 1)]
            return (layer, page, lax.bitwise_and(blk, 1), h)
        return pl.BlockSpec((None, None, BLOCK, HEAD_DIM), index)

    ospec = pl.BlockSpec((None, None, SLOTS_PER_STEP * BLOCK, HEAD_DIM), lambda b, h, s, pt, idx: (b, h, s, 0))
    grid_spec = pltpu.PrefetchScalarGridSpec(
        num_scalar_prefetch=2,
        grid=(nseq, KV_HEADS, steps),
        in_specs=[blk_spec(n) for n in range(SLOTS_PER_STEP)] * 2,
        out_specs=(ospec, ospec),
    )
    shape = jax.ShapeDtypeStruct((nseq, KV_HEADS, ns * BLOCK, HEAD_DIM), F32)
    return pl.pallas_call(
        _gather_blocks_kernel,
        out_shape=(shape, shape),
        grid_spec=grid_spec,
        compiler_params=_params(("arbitrary", "arbitrary", "arbitrary")),
        name="gather_selected",
    )(page_table.reshape(-1), sel_idx.reshape(-1), *([pool_k] * SLOTS_PER_STEP), *([pool_v] * SLOTS_PER_STEP))


def _pad_rows(parts, total):
    n = sum(p.shape[0] for p in parts)
    if total > n:
        parts = parts + [jnp.zeros((total - n, parts[0].shape[1]), parts[0].dtype)]
    return jnp.concatenate(parts, axis=0)


def _decode_attend_kernel(qr_ref, oc_ref, gl_ref, ksn_ref, vsn_ref, kwn_ref, vwn_ref, sk_ref, sv_ref, wk_ref, wv_ref,
                          o_ref, wko_ref, wvo_ref, *, win):
    gates = _sigmoid(gl_ref[...])
    n_sel = sk_ref.shape[1]
    kidx = lax.broadcasted_iota(jnp.int32, (QROWS, SEL_ROWS), 1)
    widx = lax.broadcasted_iota(jnp.int32, (QROWS, WIN_ROWS), 1)
    for h in range(KV_HEADS):
        hs = slice(h * HEAD_DIM, (h + 1) * HEAD_DIM)
        q = _group_queries(qr_ref, h)
        k_all = _pad_rows([sk_ref[h], _rows_to_tile([ksn_ref[:, hs]])], SEL_ROWS)
        v_all = _pad_rows([sv_ref[h], _rows_to_tile([vsn_ref[:, hs]])], SEL_ROWS)
        p = _softmax_masked(_nt_f32(q, k_all) * SCALE, kidx <= n_sel)
        o_sel = _dot_f32(p, v_all)
        kw_all = _pad_rows([wk_ref[:, hs], _rows_to_tile([kwn_ref[:, hs]])], WIN_ROWS)
        vw_all = _pad_rows([wv_ref[:, hs], _rows_to_tile([vwn_ref[:, hs]])], WIN_ROWS)
        p = _softmax_masked(_nt_f32(q, kw_all) * SCALE, (widx >= 1) & (widx <= win))
        o_win = _dot_f32(p, vw_all)
        for g in range(GQA):
            head = h * GQA + g
            cols = slice(head * HEAD_DIM, (head + 1) * HEAD_DIM)
            o = (gates[:, head:head + 1] * oc_ref[:, cols]
                 + gates[:, N_HEADS + head:N_HEADS + head + 1] * o_sel[g:g + 1, :]
                 + gates[:, 2 * N_HEADS + head:2 * N_HEADS + head + 1] * o_win[g:g + 1, :])
            o_ref[:, cols] = o
    last = lax.broadcasted_iota(jnp.int32, (win, KV_WIDTH), 0) == win - 1
    wko_ref[...] = jnp.where(last, jnp.broadcast_to(kwn_ref[...], (win, KV_WIDTH)),
                             pltpu.roll(wk_ref[...], win - 1, 0))
    wvo_ref[...] = jnp.where(last, jnp.broadcast_to(vwn_ref[...], (win, KV_WIDTH)),
                             pltpu.roll(wv_ref[...], win - 1, 0))


def decode_attend(qr, o_cmp, gl, ks_new, vs_new, kw_new, vw_new, sel_k, sel_v, win_k, win_v, layer):
    nseq = qr.shape[0]
    win = win_k.shape[2]
    n_sel = sel_k.shape[2]
    assert win == WINDOW
    assert n_sel + QROWS <= SEL_ROWS and win + QROWS <= WIN_ROWS
    row = lambda w: pl.BlockSpec((None, 1, w), lambda b: (b, 0, 0))
    sspec = pl.BlockSpec((None, KV_HEADS, n_sel, HEAD_DIM), lambda b: (b, 0, 0, 0))
    wspec_in = pl.BlockSpec((None, None, win, KV_WIDTH), lambda b: (layer, b, 0, 0))
    wspec_out = pl.BlockSpec((None, win, KV_WIDTH), lambda b: (b, 0, 0))
    r3 = lambda t: t.reshape(nseq, 1, t.shape[-1])
    o, wk_new, wv_new = pl.pallas_call(
        functools.partial(_decode_attend_kernel, win=win),
        out_shape=(jax.ShapeDtypeStruct((nseq, 1, ATT_WIDTH), F32),
                   jax.ShapeDtypeStruct((nseq, win, KV_WIDTH), F32),
                   jax.ShapeDtypeStruct((nseq, win, KV_WIDTH), F32)),
        grid=(nseq,),
        in_specs=[row(ATT_WIDTH), row(ATT_WIDTH), row(LANE), row(KV_WIDTH), row(KV_WIDTH), row(KV_WIDTH),
                  row(KV_WIDTH), sspec, sspec, wspec_in, wspec_in],
        out_specs=(row(ATT_WIDTH), wspec_out, wspec_out),
        compiler_params=_params(("arbitrary",)),
        name="decode_attend",
    )(r3(qr), r3(o_cmp), r3(gl), r3(ks_new), r3(vs_new), r3(kw_new), r3(vw_new), sel_k, sel_v, win_k, win_v)
    return o.reshape(nseq, ATT_WIDTH), wk_new, wv_new


def decode_mixer(hn, layer, past_len, state_hg, state_conv, page_table, pool_ck, pool_cv, pool_sk, pool_sv,
                 win_k, win_v, w_in, w_out, lower, hg_norm_g, conv_w, q_norm_g, k_norm_g, cmp_pos_w):
    nseq = hn.shape[0]
    assert hn.dtype == F32
    z = matmul_rows([hn], w_in, IN_MAIN, 512, lead=(layer,))
    gl = matmul_rows([hn], _gate_weights(w_in, layer), LANE, LANE)
    o_hg, o_cv, s_hg, s_cv = recurrent_step(z, layer, state_hg, state_conv, lower[layer], hg_norm_g[layer],
                                            conv_w[layer])
    cos2, sin2 = rope_tables(jnp.full((nseq,), past_len, jnp.int32))
    qn, qr, k_c, k_s, k_w = qk_prep(z, q_norm_g[layer], k_norm_g[layer], cos2, sin2, nseq, q_dtype=F32)
    v_c = z[:, OFF_VC:OFF_VC + KV_WIDTH]
    v_s = z[:, OFF_VS:OFF_VS + KV_WIDTH]
    v_w = z[:, OFF_VW:OFF_VW + KV_WIDTH]
    ck, cv = compress_pages(pool_ck, pool_cv, page_table, layer, cmp_pos_w[layer])
    o_cmp, sel = decode_select(qn, ck, cv)
    sel_k, sel_v = gather_selected(sel, page_table, pool_sk, pool_sv, layer)
    o_att, wk, wv = decode_attend(qr, o_cmp, gl, k_s, v_s, k_w, v_w, sel_k, sel_v, win_k, win_v, layer)
    y = matmul_rows([o_hg, o_cv, o_att], w_out, D_MODEL, 512, lead=(layer,))
    win = wk.shape[1]
    states = (_kv4(k_c, nseq, 1), _kv4(v_c, nseq, 1), _kv4(k_s, nseq, 1), _kv4(v_s, nseq, 1),
              wk.reshape(nseq, win, KV_HEADS, HEAD_DIM), wv.reshape(nseq, win, KV_HEADS, HEAD_DIM), s_hg, s_cv)
    return y, states


MOE_TILE_PROMPT = (1024, 256)
MOE_TILE_DECODE = (128, 128)


def kernel(x_prompt, x_sample, cache_cmp_k, cache_cmp_v, cache_sel_k, cache_sel_v, cache_win_k, cache_win_v,
           state_hg, state_conv, page_table, ln1_g, ln2_g, w_in, w_out, hg_lb, hg_norm_g, conv_w, q_norm_g,
           k_norm_g, cmp_pos_w, ffn_w1, ffn_w3, ffn_w2, router_w, moe_w1, moe_w3, moe_w2):
    depth = w_in.shape[0]
    batch, seq, d = x_prompt.shape
    nseq = x_sample.shape[0]
    assert x_sample.shape[1] == 1
    past_len = page_table.shape[1] * PAGE_SIZE
    keep = min(WINDOW, past_len)
    assert seq >= keep
    lb_sm = jax.nn.softmax(hg_lb.astype(F32), axis=0)
    lower = jnp.cumsum(lb_sm, axis=0) - lb_sm[0]

    def pool(c):
        return c.reshape(c.shape[0], c.shape[1], c.shape[2], KV_WIDTH)

    pools = [pool(c) for c in (cache_cmp_k, cache_cmp_v, cache_sel_k, cache_sel_v, cache_win_k, cache_win_v)]
    mix_w = (w_in, w_out, lower, hg_norm_g, conv_w, q_norm_g, k_norm_g, cmp_pos_w)

    def prompt_mix(hn, l):
        return prompt_mixer(hn, l, batch, seq, keep, *mix_w)

    def decode_mix(hn, l):
        return decode_mixer(hn, l, past_len, state_hg, state_conv, page_table, *pools, *mix_w)

    outs = []
    for x, mix, moe_tile, act in ((x_prompt.reshape(batch * seq, d), prompt_mix, MOE_TILE_PROMPT, BF16),
                                  (x_sample.reshape(nseq, d), decode_mix, MOE_TILE_DECODE, F32)):
        states = []
        ffn_out = None
        for l in range(depth):
            if ffn_out is None:
                hn = rmsnorm_rows(x, ln1_g[l], act)
            else:
                x, hn = add_rmsnorm_rows(x, ffn_out, ln1_g[l], act)
            m, st = mix(hn, l)
            states.append(st)
            x, hn2 = add_rmsnorm_rows(x, m, ln2_g[l], act)
            if l % 2 == 0:
                ffn_out = ffn_rows(hn2, ffn_w1, ffn_w3, ffn_w2, lead=(l // 2,))
            else:
                ffn_out = moe_ffn(x, hn2, ln2_g[l], router_w[l // 2], moe_w1, moe_w3, moe_w2, l // 2, *moe_tile)
        outs.append((add_rows(x, ffn_out), states))

    (yp, sp), (ys, ss) = outs

    def stk(states, i):
        return jnp.stack([s[i] for s in states])

    return ((yp.reshape(batch, seq, d), ys.reshape(nseq, 1, d))
            + tuple(stk(sp, i) for i in range(8)) + tuple(stk(ss, i) for i in range(8)))
```
